```python
import jax, jax.numpy as jnp
from jax import lax
import numpy as np

D_MODEL = 2048
BATCH = 4
SEQ = 4096
DEPTH = 1
DEC_BATCH = 32
DEC_SEQ = 1
PAST_LEN = 16384
PAGE_SIZE = 128

NSA_HEADS = 8
NSA_GROUPS = 2
NSA_HPG = NSA_HEADS // NSA_GROUPS
NSA_DH = 128
NSA_W = NSA_HEADS * NSA_DH
NSA_KV_W = 2 * NSA_GROUPS * NSA_DH
N_NSA_BRANCH = 3
CMP_LEN = 32
CMP_STRIDE = 16
CMP_HID = 256
SEL_LEN = 64
N_SEL = 16
WINDOW = 512
NSA_SCALE = NSA_DH ** -0.5
SB_HEADS = 8
SB_DH = 64
SB_W = SB_HEADS * SB_DH
SB_SCALE = SB_DH ** -0.5
MEM_HEADS = 4
MEM_DH = 128
MEM_W = MEM_HEADS * MEM_DH
N_MEM = 256
MEM_SCALE = MEM_DH ** -0.5
N_MERGE = 3
ROT_DIM = NSA_DH // 4
ROPE_THETA = 500000.0
QBLK = 128
RMS_EPS = 1e-6
FORCE_SCORE = 1e4
NEG = -1e30
IN_SIZES = (NSA_W, NSA_KV_W, NSA_KV_W, NSA_KV_W, N_NSA_BRANCH * NSA_HEADS, NSA_W,
            SB_W, SB_W, SB_W, SB_W, MEM_W, MEM_W, N_MERGE * D_MODEL)
IN_W = 2 * NSA_W + 3 * NSA_KV_W + N_NSA_BRANCH * NSA_HEADS + 4 * SB_W + 2 * MEM_W + N_MERGE * D_MODEL

kernel_name = 'nsa_stickbreak_memory_hybrid_step'


def _rmsnorm(x, g):
    xf = x.astype(jnp.float32)
    y = xf * lax.rsqrt(jnp.mean(xf * xf, axis=-1, keepdims=True) + RMS_EPS)
    return (y * g.astype(jnp.float32)).astype(x.dtype)


def _rope(x, pos):
    half = ROT_DIM // 2
    inv = 1.0 / (ROPE_THETA ** (jnp.arange(half, dtype=jnp.float32) * (2.0 / ROT_DIM)))
    ang = pos.astype(jnp.float32)[:, None] * inv[None, :]
    cos = jnp.cos(ang)[None, :, None, :]
    sin = jnp.sin(ang)[None, :, None, :]
    xr = x[..., :ROT_DIM].astype(jnp.float32)
    x1, x2 = xr[..., :half], xr[..., half:]
    rot = jnp.concatenate([x1 * cos - x2 * sin, x2 * cos + x1 * sin], axis=-1).astype(x.dtype)
    return jnp.concatenate([rot, x[..., ROT_DIM:]], axis=-1)


def _rope_kv(kv, pos):
    return jnp.stack([_rope(kv[:, :, 0], pos), kv[:, :, 1]], axis=2)


def _project(x, pos, g_norm, w_in):
    b, t = x.shape[:2]
    h = _rmsnorm(x, g_norm)
    offs = np.cumsum(IN_SIZES)[:-1].tolist()
    (q, kv_c, kv_s, kv_w, g_nsa, z_nsa, q_sb, k_sb, v_sb, z_sb, q_mem, z_mem, g_mrg) = jnp.split(h @ w_in, offs, axis=-1)
    q = _rope(q.reshape(b, t, NSA_HEADS, NSA_DH), pos).reshape(b, t, NSA_GROUPS, NSA_HPG, NSA_DH)
    kv_shape = (b, t, 2, NSA_GROUPS, NSA_DH)
    kv_c = _rope_kv(kv_c.reshape(kv_shape), pos)
    kv_s = _rope_kv(kv_s.reshape(kv_shape), pos)
    kv_w = _rope_kv(kv_w.reshape(kv_shape), pos)
    sb_shape = (b, t, SB_HEADS, SB_DH)
    kv_sb = jnp.stack([k_sb.reshape(sb_shape), v_sb.reshape(sb_shape)], axis=2)
    return (q, kv_c, kv_s, kv_w, g_nsa, z_nsa, q_sb.reshape(sb_shape), kv_sb, z_sb,
            q_mem.reshape(b, t, MEM_HEADS, MEM_DH), z_mem, g_mrg)


def _compress(kv, pe, w1, b1, w2):
    b, l = kv.shape[:2]
    r = CMP_LEN // CMP_STRIDE
    n_chunk = l // CMP_STRIDE
    nc = n_chunk - r + 1
    chunks = kv[:, :n_chunk * CMP_STRIDE].reshape(b, n_chunk, CMP_STRIDE, 2, NSA_GROUPS, NSA_DH)
    w1c = w1.reshape(2, r, CMP_STRIDE, NSA_DH, CMP_HID)
    hid = (b1 + jnp.einsum('lcd,cldh->ch', pe, w1))[None, None, :, None, :]
    for i in range(r):
        part = jnp.einsum('bnlcgd,cldh->bncgh', chunks, w1c[:, i])
        hid = hid + part[:, i:i + nc]
    return jnp.einsum('bncgh,chd->bncgd', jax.nn.gelu(hid), w2)


def _cmp_attn(q, kvc, qpos):
    nc = kvc.shape[1]
    ends = jnp.arange(nc) * CMP_STRIDE + CMP_LEN - 1
    mask = ends[None, :] <= qpos[:, None]
    s = jnp.einsum('btghd,bngd->bghtn', q, kvc[:, :, 0]).astype(jnp.float32) * NSA_SCALE
    p = jnp.where(mask, jax.nn.softmax(jnp.where(mask, s, NEG), axis=-1), 0.0)
    o = jnp.einsum('bghtn,bngd->btghd', p.astype(kvc.dtype), kvc[:, :, 1])
    return o, p.sum(axis=2)


def _select(p_grp, qpos, n_blk):
    nc = p_grp.shape[-1]
    r = SEL_LEN // CMP_STRIDE
    lpad = CMP_LEN // CMP_STRIDE - 1
    rpad = r * (n_blk + 1) - lpad - nc
    pp = jnp.pad(p_grp, ((0, 0), (0, 0), (0, 0), (lpad, rpad))).reshape(*p_grp.shape[:3], n_blk + 1, r)
    p_slc = pp[..., :n_blk, :].sum(-1) + pp[..., 1:, :lpad].sum(-1)
    j = jnp.arange(n_blk)[None, :]
    cur = (qpos // SEL_LEN)[:, None]
    valid = j * SEL_LEN <= qpos[:, None]
    forced = (j == 0) | (j == cur) | (j == cur - 1)
    score = jnp.where(valid, jnp.where(forced, FORCE_SCORE, p_slc), -jnp.inf)
    vals, idx = lax.top_k(score, min(N_SEL, n_blk))
    return idx.transpose(0, 2, 1, 3), jnp.isfinite(vals).transpose(0, 2, 1, 3)


def _sel_attn(q, ks, vs, idx, valid, qpos):
    kpos = idx[..., None] * SEL_LEN + jnp.arange(SEL_LEN)
    mask = valid[..., None] & (kpos <= qpos[None, :, None, None, None])
    s = jnp.einsum('btghd,btgkrd->btghkr', q, ks).astype(jnp.float32) * NSA_SCALE
    p = jax.nn.softmax(jnp.where(mask[:, :, :, None], s, NEG), axis=(-2, -1))
    return jnp.einsum('btghkr,btgkrd->btghd', p.astype(vs.dtype), vs)


def _win_attn(q, kv, kpos, qpos):
    mask = (kpos[None, :] <= qpos[:, None]) & (qpos[:, None] - kpos[None, :] < WINDOW) & (kpos[None, :] >= 0)
    s = jnp.einsum('btghd,bkgd->bghtk', q, kv[:, :, 0]).astype(jnp.float32) * NSA_SCALE
    p = jax.nn.softmax(jnp.where(mask, s, NEG), axis=-1)
    return jnp.einsum('bghtk,bkgd->btghd', p.astype(kv.dtype), kv[:, :, 1])


def _sb_block(q, k, v, qpos, kpos, c):
    z = jnp.einsum('bthd,bkhd->bhtk', q, k).astype(jnp.float32) * SB_SCALE
    mask = kpos[None, :] < qpos[:, None]
    m = jnp.where(mask, jax.nn.log_sigmoid(-z), 0.0)
    excl = lax.cumsum(m, axis=3, reverse=True) - m
    a = jnp.where(mask, jnp.exp(jax.nn.log_sigmoid(z) + excl + c[..., None]), 0.0)
    o = jnp.einsum('bhtk,bkhd->bthd', a.astype(v.dtype), v).astype(jnp.float32)
    return o, c + m.sum(-1)


def _mem_attn(q, mkv):
    s = jnp.einsum('bthd,bnhd->bhtn', q, mkv[:, :, 0]).astype(jnp.float32) * MEM_SCALE
    p = jax.nn.softmax(s, axis=-1)
    return jnp.einsum('bhtn,bnhd->bthd', p.astype(mkv.dtype), mkv[:, :, 1])


def _mem_kv(mem, g_mem, w_mem_kv):
    b, n = mem.shape[:2]
    return (_rmsnorm(mem, g_mem) @ w_mem_kv).reshape(b, n, 2, MEM_HEADS, MEM_DH)


def _merge(x, o_cmp, o_sel, o_win, g_nsa, z_nsa, o_sb, z_sb, o_mem, z_mem, g_mrg,
           w_br_nsa, w_br_sb, w_br_mem, w_out):
    b, t = x.shape[:2]
    gn = jax.nn.sigmoid(g_nsa).reshape(b, t, N_NSA_BRANCH, NSA_GROUPS, NSA_HPG, 1).astype(o_cmp.dtype)
    o_nsa = gn[:, :, 0] * o_cmp + gn[:, :, 1] * o_sel + gn[:, :, 2] * o_win
    br_nsa = (o_nsa.reshape(b, t, NSA_W).astype(x.dtype) * jax.nn.silu(z_nsa)) @ w_br_nsa
    br_sb = (o_sb.reshape(b, t, SB_W).astype(x.dtype) * jax.nn.silu(z_sb)) @ w_br_sb
    br_mem = (o_mem.reshape(b, t, MEM_W).astype(x.dtype) * jax.nn.silu(z_mem)) @ w_br_mem
    gm = jax.nn.sigmoid(g_mrg).reshape(b, t, N_MERGE, D_MODEL)
    mixed = gm[:, :, 0] * br_nsa + gm[:, :, 1] * br_sb + gm[:, :, 2] * br_mem
    return x + mixed @ w_out


def _prompt_layer(x, mem, g_norm, w_in, pe_cmp, w1_cmp, b1_cmp, w2_cmp, g_mem, w_mem_kv,
                  w_br_nsa, w_br_sb, w_br_mem, w_out):
    b, t = x.shape[:2]
    pos = jnp.arange(t, dtype=jnp.int32)
    (q, kv_cmp, kv_sel, kv_win, g_nsa, z_nsa, q_sb, kv_sb, z_sb, q_mem, z_mem, g_mrg) = _project(x, pos, g_norm, w_in)
    kvc = _compress(kv_cmp, pe_cmp, w1_cmp, b1_cmp, w2_cmp)
    n_blk = t // SEL_LEN
    sel_blocks = kv_sel.reshape(b, n_blk, SEL_LEN, 2, NSA_GROUPS, NSA_DH)
    win_pad = jnp.pad(kv_win, ((0, 0), (WINDOW, 0), (0, 0), (0, 0), (0, 0)))
    bi = jnp.arange(b)[:, None, None, None]
    gi = jnp.arange(NSA_GROUPS)[None, None, :, None]

    def qblock(i):
        q0 = i * QBLK
        qpos = q0 + jnp.arange(QBLK, dtype=jnp.int32)
        qb = lax.dynamic_slice_in_dim(q, q0, QBLK, axis=1)
        o_c, p_grp = _cmp_attn(qb, kvc, qpos)
        idx, valid = _select(p_grp, qpos, n_blk)
        g = sel_blocks[bi, idx, :, :, gi]
        o_s = _sel_attn(qb, g[..., 0, :], g[..., 1, :], idx, valid, qpos)
        kw = lax.dynamic_slice_in_dim(win_pad, q0, WINDOW + QBLK, axis=1)
        o_w = _win_attn(qb, kw, q0 - WINDOW + jnp.arange(WINDOW + QBLK), qpos)
        qs = lax.dynamic_slice_in_dim(q_sb, q0, QBLK, axis=1)
        o_b, _ = _sb_block(qs, kv_sb[:, :, 0], kv_sb[:, :, 1], qpos, pos,
                           jnp.zeros((b, SB_HEADS, QBLK), jnp.float32))
        return o_c, o_s, o_w, o_b

    outs = lax.map(qblock, jnp.arange(t // QBLK))
    o_cmp, o_sel, o_win, o_sb = [jnp.moveaxis(o, 0, 1).reshape(b, t, *o.shape[3:]) for o in outs]
    mem_kv = _mem_kv(mem, g_mem, w_mem_kv)
    o_mem = _mem_attn(q_mem, mem_kv)
    x = _merge(x, o_cmp, o_sel, o_win, g_nsa, z_nsa, o_sb, z_sb, o_mem, z_mem, g_mrg,
               w_br_nsa, w_br_sb, w_br_mem, w_out)
    return x, kv_cmp, kv_sel, kv_sb, kv_win[:, t - min(WINDOW, t):], mem_kv


def _sample_layer(x, l, cache_cmp_kv, cache_sel_kv, cache_sb_kv, win_kv, mem_kv, page_table,
                  g_norm, w_in, pe_cmp, w1_cmp, b1_cmp, w2_cmp, w_br_nsa, w_br_sb, w_br_mem, w_out):
    b, t = x.shape[:2]
    n_pages = page_table.shape[1]
    past = n_pages * PAGE_SIZE
    pos = past + jnp.arange(t, dtype=jnp.int32)
    (q, kv_cmp, kv_sel, kv_win, g_nsa, z_nsa, q_sb, kv_sb, z_sb, q_mem, z_mem, g_mrg) = _project(x, pos, g_norm, w_in)
    past_cmp = cache_cmp_kv[l, page_table].reshape(b, past, 2, NSA_GROUPS, NSA_DH)
    kvc = _compress(jnp.concatenate([past_cmp, kv_cmp], axis=1), pe_cmp, w1_cmp, b1_cmp, w2_cmp)
    o_cmp, p_grp = _cmp_attn(q, kvc, pos)
    npb = past // SEL_LEN
    bpp = PAGE_SIZE // SEL_LEN
    nt = -(-t // SEL_LEN)
    idx, valid = _select(p_grp, pos, npb + nt)
    bi = jnp.arange(b)[:, None, None, None]
    gi = jnp.arange(NSA_GROUPS)[None, None, :, None]
    rr = jnp.arange(SEL_LEN)
    jp = jnp.minimum(idx, npb - 1)
    phys = page_table[bi, jp // bpp]
    from_pages = cache_sel_kv[l, phys[..., None], (jp % bpp)[..., None] * SEL_LEN + rr, :, gi[..., None]]
    tail = jnp.pad(kv_sel, ((0, 0), (0, nt * SEL_LEN - t), (0, 0), (0, 0), (0, 0))).reshape(
        b, nt, SEL_LEN, 2, NSA_GROUPS, NSA_DH)
    jt = jnp.clip(idx - npb, 0, nt - 1)
    from_tail = tail[bi[..., None], jt[..., None], rr, :, gi[..., None]]
    g = jnp.where((idx >= npb)[..., None, None, None], from_tail, from_pages)
    o_sel = _sel_attn(q, g[..., 0, :], g[..., 1, :], idx, valid, pos)
    wb = win_kv.shape[1]
    kw = jnp.concatenate([win_kv, kv_win], axis=1)
    o_win = _win_attn(q, kw, past - wb + jnp.arange(wb + t), pos)
    o_new, c0 = _sb_block(q_sb, kv_sb[:, :, 0], kv_sb[:, :, 1], pos, pos,
                          jnp.zeros((b, SB_HEADS, t), jnp.float32))

    def sb_page(carry, pidx):
        acc, c = carry
        kvp = cache_sb_kv[l, page_table[:, pidx]]
        o, c = _sb_block(q_sb, kvp[:, :, 0], kvp[:, :, 1], pos, pidx * PAGE_SIZE + jnp.arange(PAGE_SIZE), c)
        return (acc + o, c), None

    (o_sb, _), _ = lax.scan(sb_page, (o_new, c0), jnp.arange(n_pages), reverse=True)
    o_mem = _mem_attn(q_mem, mem_kv)
    x = _merge(x, o_cmp, o_sel, o_win, g_nsa, z_nsa, o_sb, z_sb, o_mem, z_mem, g_mrg,
               w_br_nsa, w_br_sb, w_br_mem, w_out)
    return x, kv_cmp, kv_sel, kv_sb, kw[:, t:]


def setup_inputs(seed: int = 0) -> dict:
    key = jax.random.key(seed)
    ks = jax.random.split(key, 24)
    f32 = jnp.float32
    n_pages = PAST_LEN // PAGE_SIZE
    n_used = DEC_BATCH * n_pages
    n_pool = n_used + n_used // 4
    wb = min(WINDOW, PAST_LEN)

    def nrm(k, shape, scale):
        return jax.random.normal(k, shape, f32) * scale

    page_table = jax.random.permutation(ks[0], n_pool)[:n_used].reshape(DEC_BATCH, n_pages).astype(jnp.int32)
    return {
        'x_prompt': nrm(ks[1], (BATCH, SEQ, D_MODEL), 1.0),
        'x_sample': nrm(ks[2], (DEC_BATCH, DEC_SEQ, D_MODEL), 1.0),
        'mem_prompt': nrm(ks[3], (BATCH, N_MEM, D_MODEL), 1.0),
        'cache_cmp_kv': nrm(ks[4], (DEPTH, n_pool, PAGE_SIZE, 2, NSA_GROUPS, NSA_DH), 1.0),
        'cache_sel_kv': nrm(ks[5], (DEPTH, n_pool, PAGE_SIZE, 2, NSA_GROUPS, NSA_DH), 1.0),
        'cache_sb_kv': nrm(ks[6], (DEPTH, n_pool, PAGE_SIZE, 2, SB_HEADS, SB_DH), 1.0),
        'cache_win_kv': nrm(ks[7], (DEPTH, DEC_BATCH, wb, 2, NSA_GROUPS, NSA_DH), 1.0),
        'cache_mem_kv': nrm(ks[8], (DEPTH, DEC_BATCH, N_MEM, 2, MEM_HEADS, MEM_DH), 1.0),
        'page_table': page_table,
        'g_norm': 1.0 + nrm(ks[9], (DEPTH, D_MODEL), 0.02),
        'w_in': nrm(ks[10], (DEPTH, D_MODEL, IN_W), D_MODEL ** -0.5),
        'pe_cmp': nrm(ks[11], (DEPTH, CMP_LEN, 2, NSA_DH), 0.1),
        'w1_cmp': nrm(ks[12], (DEPTH, 2, CMP_LEN, NSA_DH, CMP_HID), (CMP_LEN * NSA_DH) ** -0.5),
        'b1_cmp': nrm(ks[13], (DEPTH, 2, CMP_HID), 0.01),
        'w2_cmp': nrm(ks[14], (DEPTH, 2, CMP_HID, NSA_DH), CMP_HID ** -0.5),
        'g_mem': 1.0 + nrm(ks[15], (DEPTH, D_MODEL), 0.02),
        'w_mem_kv': nrm(ks[16], (DEPTH, D_MODEL, 2 * MEM_W), D_MODEL ** -0.5),
        'w_br_nsa': nrm(ks[17], (DEPTH, NSA_W, D_MODEL), NSA_W ** -0.5),
        'w_br_sb': nrm(ks[18], (DEPTH, SB_W, D_MODEL), SB_W ** -0.5),
        'w_br_mem': nrm(ks[19], (DEPTH, MEM_W, D_MODEL), MEM_W ** -0.5),
        'w_out': nrm(ks[20], (DEPTH, D_MODEL, D_MODEL), D_MODEL ** -0.5),
        'g_final': 1.0 + nrm(ks[21], (D_MODEL,), 0.02),
    }


def reference(x_prompt, x_sample, mem_prompt, cache_cmp_kv, cache_sel_kv, cache_sb_kv, cache_win_kv,
              cache_mem_kv, page_table, g_norm, w_in, pe_cmp, w1_cmp, b1_cmp, w2_cmp, g_mem, w_mem_kv,
              w_br_nsa, w_br_sb, w_br_mem, w_out, g_final):
    xp, xs = x_prompt, x_sample
    p_cmp, p_sel, p_sb, p_win, p_mem = [], [], [], [], []
    s_cmp, s_sel, s_sb, s_win = [], [], [], []
    for l in range(DEPTH):
        xp, a, bsel, c, d, e = _prompt_layer(
            xp, mem_prompt, g_norm[l], w_in[l], pe_cmp[l], w1_cmp[l], b1_cmp[l], w2_cmp[l],
            g_mem[l], w_mem_kv[l], w_br_nsa[l], w_br_sb[l], w_br_mem[l], w_out[l])
        p_cmp.append(a); p_sel.append(bsel); p_sb.append(c); p_win.append(d); p_mem.append(e)
        xs, a, bsel, c, d = _sample_layer(
            xs, l, cache_cmp_kv, cache_sel_kv, cache_sb_kv, cache_win_kv[l], cache_mem_kv[l], page_table,
            g_norm[l], w_in[l], pe_cmp[l], w1_cmp[l], b1_cmp[l], w2_cmp[l],
            w_br_nsa[l], w_br_sb[l], w_br_mem[l], w_out[l])
        s_cmp.append(a); s_sel.append(bsel); s_sb.append(c); s_win.append(d)
    y_prompt = _rmsnorm(xp, g_final)
    y_sample = _rmsnorm(xs, g_final)
    return (y_prompt, y_sample,
            jnp.stack(p_cmp), jnp.stack(p_sel), jnp.stack(p_sb), jnp.stack(p_win), jnp.stack(p_mem),
            jnp.stack(s_cmp), jnp.stack(s_sel), jnp.stack(s_sb), jnp.stack(s_win))
```

```python
import functools

import numpy as np
import jax
import jax.numpy as jnp
from jax import lax
from jax.experimental import pallas as pl
from jax.experimental.pallas import tpu as pltpu

F32 = jnp.float32
BF16 = jnp.bfloat16

D_MODEL = 2048
PAGE = 128
NSA_HEADS = 8
NSA_GROUPS = 2
NSA_HPG = NSA_HEADS // NSA_GROUPS
DH = 128
N_NSA_BRANCH = 3
CMP_LEN = 32
CMP_STRIDE = 16
CMP_HID = 256
SEL_LEN = 64
N_SEL = 16
WINDOW = 512
NSA_SCALE = DH ** -0.5
SB_HEADS = 8
SB_DH = 64
SB_SCALE = SB_DH ** -0.5
MEM_HEADS = 4
MEM_SCALE = DH ** -0.5
ROT_DIM = DH // 4
ROPE_THETA = 500000.0
QBLK = 128
RMS_EPS = 1e-6
FORCE_SCORE = 1e4
NEG = -1e30

LANES = 128

CB_Q = 0
CB_KVC = 8
CB_KVS = 12
CB_KVW = 16
CB_ZNSA = 20
CB_QSB = 28
CB_KSB = 32
CB_VSB = 36
CB_ZSB = 40
CB_QMEM = 44
CB_ZMEM = 48
CB_GMRG = 52
CB_GNSA = 100
PROJ_TN = 512
PROJ_W = 104 * LANES
G_NSA_COL0 = 2560
N_GNSA = N_NSA_BRANCH * NSA_HEADS

VMEM_LIMIT = 56 * 1024 * 1024


def _cparams(sem):
    return pltpu.CompilerParams(dimension_semantics=sem, vmem_limit_bytes=VMEM_LIMIT)


def _bdot(a, b):
    return jnp.dot(a.astype(BF16), b.astype(BF16), preferred_element_type=F32)


def _bdot_nt(a, b):
    return lax.dot_general(a.astype(BF16), b.astype(BF16), (((1,), (1,)), ((), ())),
                           preferred_element_type=F32)


def _split3(x):
    p1 = x.astype(BF16)
    r1 = x - p1.astype(F32)
    p2 = r1.astype(BF16)
    r2 = r1 - p2.astype(F32)
    return p1, p2, r2.astype(BF16)


def _sigmoid(x):
    return 1.0 / (1.0 + jnp.exp(-x))


def _silu(x):
    return x * _sigmoid(x)


def _log_sigmoid(x):
    return jnp.minimum(x, 0.0) - jnp.log1p(jnp.exp(-jnp.abs(x)))


def _gelu_tanh(x):
    return 0.5 * x * (1.0 + jnp.tanh(np.sqrt(2.0 / np.pi).astype(np.float32) * (x + 0.044715 * (x * x * x))))


def _softmax_rows(s):
    m = jnp.max(s, axis=-1, keepdims=True)
    e = jnp.exp(s - m)
    return e / jnp.sum(e, axis=-1, keepdims=True)


def _proj_kernel(x_ref, g_ref, w_ref, c_ref, s1_ref, s2_ref, o_ref, hn_ref, *, rope_chunks):
    j = pl.program_id(1)

    @pl.when(j == 0)
    def _():
        x = x_ref[...]
        ms = jnp.mean(x * x, axis=-1, keepdims=True)
        hn_ref[...] = (x * lax.rsqrt(ms + RMS_EPS) * g_ref[...]).astype(BF16)

    acc = jnp.dot(hn_ref[...], w_ref[...], preferred_element_type=F32)
    n_chunks = acc.shape[1] // LANES

    def store(n_rope):
        for k in range(n_chunks):
            xc = acc[:, k * LANES:(k + 1) * LANES]
            if k < n_rope:
                xc = (xc * c_ref[...] + pltpu.roll(xc, LANES - ROT_DIM // 2, 1) * s1_ref[...]
                      + pltpu.roll(xc, ROT_DIM // 2, 1) * s2_ref[...])
            o_ref[:, k * LANES:(k + 1) * LANES] = xc

    for jj, n_rope in enumerate(rope_chunks):
        pl.when(j == jj)(functools.partial(store, n_rope))
    pl.when(j >= len(rope_chunks))(functools.partial(store, 0))


def _project(x2d, g, w_bf, rope_tabs, rope_chunks, tm, tn):
    r, d = x2d.shape
    n = w_bf.shape[1]
    c_tab, s1_tab, s2_tab = rope_tabs
    nt = c_tab.shape[0] // tm
    tab_spec = pl.BlockSpec((tm, LANES), lambda i, j: (i % nt, 0))
    return pl.pallas_call(
        functools.partial(_proj_kernel, rope_chunks=rope_chunks),
        grid=(r // tm, n // tn),
        in_specs=[
            pl.BlockSpec((tm, d), lambda i, j: (i, 0)),
            pl.BlockSpec((1, d), lambda i, j: (0, 0)),
            pl.BlockSpec((d, tn), lambda i, j: (0, j)),
            tab_spec, tab_spec, tab_spec,
        ],
        out_specs=pl.BlockSpec((tm, tn), lambda i, j: (i, j)),
        out_shape=jax.ShapeDtypeStruct((r, n), F32),
        scratch_shapes=[pltpu.VMEM((tm, d), BF16)],
        compiler_params=_cparams(("arbitrary", "arbitrary")),
        name="proj",
    )(x2d, g.reshape(1, d), w_bf, c_tab, s1_tab, s2_tab)


def _rope_tables(pos):
    half = ROT_DIM // 2
    inv = 1.0 / (ROPE_THETA ** (jnp.arange(half, dtype=F32) * (2.0 / ROT_DIM)))
    ang = pos.astype(F32)[:, None] * inv[None, :]
    cos, sin = jnp.cos(ang), jnp.sin(ang)
    n = pos.shape[0]
    one = jnp.ones((n, DH - ROT_DIM), F32)
    zero = jnp.zeros((n, DH - ROT_DIM), F32)
    zh = jnp.zeros((n, half), F32)
    c_tab = jnp.concatenate([cos, cos, one], axis=1)
    s1_tab = jnp.concatenate([-sin, zh, zero], axis=1)
    s2_tab = jnp.concatenate([zh, sin, zero], axis=1)
    return c_tab, s1_tab, s2_tab


def _cmp_tail(p0, p1_next_aligned, b1, w2):
    hid = p0 + p1_next_aligned + b1
    return _bdot(_gelu_tanh(hid), w2)


def _compress_prompt_kernel(x_ref, pe_ref, w1_ref, b1_ref, w2_ref, o_ref):
    t = x_ref.shape[0]
    nch = t // CMP_STRIDE
    parts = []
    for i in range(CMP_LEN // CMP_STRIDE):
        acc = jnp.zeros((nch, CMP_HID), F32)
        for lp in range(CMP_STRIDE // 2):
            row = i * (CMP_STRIDE // 2) + lp
            xa = x_ref[pl.ds(2 * lp, nch, stride=CMP_STRIDE), :]
            xb = x_ref[pl.ds(2 * lp + 1, nch, stride=CMP_STRIDE), :]
            a = jnp.concatenate([xa, xb], axis=1) + pe_ref[row:row + 1, :]
            acc = acc + _bdot(a, w1_ref[pl.ds(row * 2 * DH, 2 * DH), :])
        parts.append(acc)
    p1_up = pltpu.roll(parts[1], nch - 1, 0)
    out = _cmp_tail(parts[0], p1_up, b1_ref[...], w2_ref[...])
    rows = lax.broadcasted_iota(jnp.int32, out.shape, 0)
    o_ref[...] = jnp.where(rows < nch - 1, out, 0.0)


def _compress_prompt(proj, b, t, pe2, w1v, b1, w2):
    nch = t // CMP_STRIDE
    return pl.pallas_call(
        _compress_prompt_kernel,
        grid=(b, 2, NSA_GROUPS),
        in_specs=[
            pl.BlockSpec((t, LANES), lambda bi, c, g: (bi, CB_KVC + 2 * c + g)),
            pl.BlockSpec((None, CMP_STRIDE, 2 * DH), lambda bi, c, g: (c, 0, 0)),
            pl.BlockSpec((None, CMP_LEN * DH, CMP_HID), lambda bi, c, g: (c, 0, 0)),
            pl.BlockSpec((None, 1, CMP_HID), lambda bi, c, g: (c, 0, 0)),
            pl.BlockSpec((None, CMP_HID, DH), lambda bi, c, g: (c, 0, 0)),
        ],
        out_specs=pl.BlockSpec((None, None, None, nch, DH), lambda bi, c, g: (bi, c, g, 0, 0)),
        out_shape=jax.ShapeDtypeStruct((b, 2, NSA_GROUPS, nch, DH), F32),
        compiler_params=_cparams(("arbitrary", "arbitrary", "arbitrary")),
        name="compress_prompt",
    )(proj, pe2, w1v, b1, w2)


def _compress_pages_kernel(pt_ref, *refs, n_pg):
    page_refs = refs[:n_pg]
    pe_ref, w1_ref, b1_ref, w2_ref, o_ref, carry_ref = refs[n_pg:]
    pg = pl.program_id(1)
    cpp = PAGE // CMP_STRIDE
    nch = n_pg * cpp
    rpp = 2 * NSA_GROUPS

    @pl.when(pg == 0)
    def _():
        carry_ref[...] = jnp.zeros_like(carry_ref)

    for cg in range(rpp):
        c = cg // NSA_GROUPS
        parts = [jnp.zeros((nch, CMP_HID), F32) for _ in range(CMP_LEN // CMP_STRIDE)]
        for lp in range(CMP_STRIDE // 2):
            xa = jnp.concatenate(
                [r[pl.ds((2 * lp) * rpp + cg, cpp, stride=CMP_STRIDE * rpp), :] for r in page_refs], axis=0)
            xb = jnp.concatenate(
                [r[pl.ds((2 * lp + 1) * rpp + cg, cpp, stride=CMP_STRIDE * rpp), :] for r in page_refs], axis=0)
            x2 = jnp.concatenate([xa, xb], axis=1)
            for i in range(CMP_LEN // CMP_STRIDE):
                row = i * (CMP_STRIDE // 2) + lp
                a = x2 + pe_ref[c, row:row + 1, :]
                parts[i] = parts[i] + _bdot(a, w1_ref[c, pl.ds(row * 2 * DH, 2 * DH), :])
        p0_down = pltpu.roll(parts[0], 1, 0)
        rows = lax.broadcasted_iota(jnp.int32, p0_down.shape, 0)
        p0_down = jnp.where(rows == 0, carry_ref[cg], p0_down)
        carry_ref[cg] = parts[0][nch - 1:nch, :]
        o_ref[cg] = _cmp_tail(p0_down, parts[1], b1_ref[c], w2_ref[c])


def _compress_pages(cache2d, page_table, pe2, w1v, b1, w2, n_pg):
    b, n_pages = page_table.shape
    rows_pp = PAGE * 2 * NSA_GROUPS
    cpp = PAGE // CMP_STRIDE
    n_chunks = n_pages * cpp

    def page_spec(k):
        return pl.BlockSpec((rows_pp, LANES), lambda bi, pg, pt: (pt[bi, pg * n_pg + k], 0))

    def const_spec(shape):
        return pl.BlockSpec(shape, lambda bi, pg, pt: (0,) * len(shape))

    grid_spec = pltpu.PrefetchScalarGridSpec(
        num_scalar_prefetch=1,
        grid=(b, n_pages // n_pg),
        in_specs=[page_spec(k) for k in range(n_pg)] + [
            const_spec((2, CMP_STRIDE, 2 * DH)),
            const_spec((2, CMP_LEN * DH, CMP_HID)),
            const_spec((2, 1, CMP_HID)),
            const_spec((2, CMP_HID, DH)),
        ],
        out_specs=pl.BlockSpec((None, 2 * NSA_GROUPS, n_pg * cpp, DH), lambda bi, pg, pt: (bi, 0, pg, 0)),
        scratch_shapes=[pltpu.VMEM((2 * NSA_GROUPS, 1, CMP_HID), F32)],
    )
    return pl.pallas_call(
        functools.partial(_compress_pages_kernel, n_pg=n_pg),
        grid_spec=grid_spec,
        out_shape=jax.ShapeDtypeStruct((b, 2 * NSA_GROUPS, n_chunks, DH), F32),
        compiler_params=_cparams(("arbitrary", "arbitrary")),
        name="compress_pages",
    )(page_table, *([cache2d] * n_pg), pe2, w1v, b1, w2)


SEL_TK = 256


def _nsa_prompt_kernel(q_ref, kc_ref, vc_ref, ks_ref, vs_ref, kw_ref, vw_ref, g_ref, z_ref, o_ref, *, t):
    gid = pl.program_id(1)
    qi = pl.program_id(2)
    q0 = qi * QBLK
    nq = NSA_HPG * QBLK
    q = q_ref[...]
    qs = jnp.concatenate([q[:, h * DH:(h + 1) * DH] for h in range(NSA_HPG)], axis=0).astype(BF16)
    qpos = q0 + (lax.broadcasted_iota(jnp.int32, (nq, 1), 0) & (QBLK - 1))

    ncp = kc_ref.shape[0]
    s = _bdot_nt(qs, kc_ref[...]) * NSA_SCALE
    ends = lax.broadcasted_iota(jnp.int32, (1, ncp), 1) * CMP_STRIDE + (CMP_LEN - 1)
    cmask = ends <= qpos
    p = jnp.where(cmask, _softmax_rows(jnp.where(cmask, s, NEG)), 0.0)
    o_cmp = _bdot(p, vc_ref[...])
    p_grp = p[0:QBLK]
    for h in range(1, NSA_HPG):
        p_grp = p_grp + p[h * QBLK:(h + 1) * QBLK]

    nblk = t // SEL_LEN
    ratio = SEL_LEN // CMP_STRIDE
    ii = lax.broadcasted_iota(jnp.int32, (ncp, nblk), 0)
    jj = lax.broadcasted_iota(jnp.int32, (ncp, nblk), 1)
    overlap = ((ii >= ratio * jj - (CMP_LEN // CMP_STRIDE - 1)) & (ii <= ratio * jj + ratio - 1)).astype(BF16)
    p_slc = sum(jnp.dot(piece, overlap, preferred_element_type=F32) for piece in _split3(p_grp))
    tq = q0 + lax.broadcasted_iota(jnp.int32, (QBLK, 1), 0)
    jb = lax.broadcasted_iota(jnp.int32, (QBLK, nblk), 1)
    cur = tq // SEL_LEN
    valid = jb * SEL_LEN <= tq
    forced = (jb == 0) | (jb == cur) | (jb == cur - 1)
    score = jnp.where(valid, jnp.where(forced, FORCE_SCORE, p_slc), -jnp.inf)
    cnt = jnp.zeros((QBLK, nblk), F32)
    for i in range(nblk):
        si = score[:, i:i + 1]
        beats = (si > score) | ((si == score) & (jb > i))
        cnt = cnt + beats.astype(F32)
    sel = ((cnt < float(min(N_SEL, nblk))) & valid).astype(BF16)

    def sel_body(kt, carry):
        m, l, acc = carry
        k0 = pl.multiple_of(kt * SEL_TK, SEL_TK)
        kk = ks_ref[pl.ds(k0, SEL_TK), :]
        vv = vs_ref[pl.ds(k0, SEL_TK), :]
        sc = _bdot_nt(qs, kk) * NSA_SCALE
        kpos = k0 + lax.broadcasted_iota(jnp.int32, (1, SEL_TK), 1)
        expand = (lax.broadcasted_iota(jnp.int32, (nblk, SEL_TK), 0)
                  == (k0 + lax.broadcasted_iota(jnp.int32, (nblk, SEL_TK), 1)) // SEL_LEN).astype(BF16)
        chosen = jnp.dot(sel, expand, preferred_element_type=F32)
        chosen = jnp.concatenate([chosen] * NSA_HPG, axis=0)
        mask = (chosen > 0.5) & (kpos <= qpos)
        sm = jnp.where(mask, sc, NEG)
        m_new = jnp.maximum(m, jnp.max(sm, axis=-1, keepdims=True))
        alpha = jnp.exp(m - m_new)
        pe = jnp.where(mask, jnp.exp(sm - m_new), 0.0)
        l = alpha * l + jnp.sum(pe, axis=-1, keepdims=True)
        acc = alpha * acc + _bdot(pe, vv)
        return m_new, l, acc

    n_kt = (q0 + QBLK + SEL_TK - 1) // SEL_TK
    m0 = jnp.full((nq, 1), NEG, F32)
    l0 = jnp.zeros((nq, 1), F32)
    a0 = jnp.zeros((nq, DH), F32)
    _, l_s, acc_s = lax.fori_loop(0, n_kt, sel_body, (m0, l0, a0))
    o_sel = acc_s / l_s

    wlen = WINDOW + QBLK
    w0 = pl.multiple_of(jnp.maximum(q0 - WINDOW, 0), QBLK)
    kw = kw_ref[pl.ds(w0, wlen), :]
    vw = vw_ref[pl.ds(w0, wlen), :]
    sw = _bdot_nt(qs, kw) * NSA_SCALE
    kposw = w0 + lax.broadcasted_iota(jnp.int32, (1, wlen), 1)
    wmask = (kposw <= qpos) & (qpos - kposw < WINDOW)
    pw = _softmax_rows(jnp.where(wmask, sw, NEG))
    o_win = _bdot(pw, vw)

    gs = _sigmoid(g_ref[...])
    lane = lax.broadcasted_iota(jnp.int32, gs.shape, 1)
    z = z_ref[...]
    for h in range(NSA_HPG):
        rows = slice(h * QBLK, (h + 1) * QBLK)
        o_h = jnp.zeros((QBLK, DH), F32)
        for br, o_br in enumerate((o_cmp, o_sel, o_win)):
            col = br * NSA_HEADS + gid * NSA_HPG + h
            gate = jnp.sum(jnp.where(lane == col, gs, 0.0), axis=1, keepdims=True)
            o_h = o_h + gate * o_br[rows]
        o_ref[:, h * DH:(h + 1) * DH] = (o_h * _silu(z[:, h * DH:(h + 1) * DH])).astype(o_ref.dtype)


def _nsa_prompt(proj, kvc, b, t):
    nqb = t // QBLK
    ncp = kvc.shape[3]
    gw = NSA_HPG * DH
    kv_spec = lambda cb: pl.BlockSpec((t, LANES), lambda bi, g, qi: (bi, cb + g))
    return pl.pallas_call(
        functools.partial(_nsa_prompt_kernel, t=t),
        grid=(b, NSA_GROUPS, nqb),
        in_specs=[
            pl.BlockSpec((QBLK, gw), lambda bi, g, qi: (bi * nqb + qi, g)),
            pl.BlockSpec((None, None, None, ncp, DH), lambda bi, g, qi: (bi, 0, g, 0, 0)),
            pl.BlockSpec((None, None, None, ncp, DH), lambda bi, g, qi: (bi, 1, g, 0, 0)),
            kv_spec(CB_KVS), kv_spec(CB_KVS + NSA_GROUPS),
            kv_spec(CB_KVW), kv_spec(CB_KVW + NSA_GROUPS),
            pl.BlockSpec((QBLK, LANES), lambda bi, g, qi: (bi * nqb + qi, CB_GNSA)),
            pl.BlockSpec((QBLK, gw), lambda bi, g, qi: (bi * nqb + qi, CB_ZNSA * LANES // gw + g)),
        ],
        out_specs=pl.BlockSpec((QBLK, gw), lambda bi, g, qi: (bi * nqb + qi, g)),
        out_shape=jax.ShapeDtypeStruct((b * t, NSA_HEADS * DH), BF16),
        compiler_params=_cparams(("arbitrary", "arbitrary", "arbitrary")),
        name="nsa_prompt",
    )(proj, kvc, kvc, proj, proj, proj, proj, proj, proj)


SB_DEAD = -110.0


def _sb_tile(zq, kk, vv, mask, c, upper):
    z = _bdot_nt(zq, kk) * SB_SCALE
    ls = _log_sigmoid(z)
    m = jnp.where(mask, ls - z, 0.0)
    m_hi = m.astype(BF16)
    m_lo = (m - m_hi.astype(F32)).astype(BF16)
    excl = (jnp.dot(m_hi, upper, preferred_element_type=F32)
            + jnp.dot(m_lo, upper, preferred_element_type=F32))
    a = jnp.where(mask, jnp.exp(ls + excl + c), 0.0)
    return _bdot(a, vv), c + jnp.sum(m, axis=-1, keepdims=True)


def _sb_prompt_kernel(q_ref, k_ref, v_ref, z_ref, o_ref):
    qi = pl.program_id(2)
    q0 = qi * QBLK
    q = q_ref[...].astype(BF16)
    qpos = q0 + lax.broadcasted_iota(jnp.int32, (QBLK, 1), 0)
    jr = lax.broadcasted_iota(jnp.int32, (QBLK, QBLK), 0)
    kc = lax.broadcasted_iota(jnp.int32, (QBLK, QBLK), 1)
    upper = (jr > kc).astype(BF16)
    nh = LANES // SB_DH

    def cond(carry):
        kt, alive = carry[0], carry[1]
        return (kt >= 0) & (alive > 0)

    def body(carry):
        kt, _, cs, os = carry
        k0 = pl.multiple_of(kt * QBLK, QBLK)
        kk = k_ref[pl.ds(k0, QBLK), :]
        vv = v_ref[pl.ds(k0, QBLK), :]
        mask = (k0 + kc) < qpos
        new_c, new_o = [], []
        cmax = None
        for h in range(nh):
            sl = slice(h * SB_DH, (h + 1) * SB_DH)
            o_t, c_t = _sb_tile(q[:, sl], kk[:, sl], vv[:, sl], mask, cs[h], upper)
            new_c.append(c_t)
            new_o.append(os[h] + o_t)
            cm = jnp.max(c_t)
            cmax = cm if cmax is None else jnp.maximum(cmax, cm)
        alive = (cmax > SB_DEAD).astype(jnp.int32)
        return kt - 1, alive, tuple(new_c), tuple(new_o)

    init = (qi, jnp.int32(1),
            tuple(jnp.zeros((QBLK, 1), F32) for _ in range(nh)),
            tuple(jnp.zeros((QBLK, SB_DH), F32) for _ in range(nh)))
    _, _, _, os = lax.while_loop(cond, body, init)
    o = jnp.concatenate(list(os), axis=1)
    o_ref[...] = (o * _silu(z_ref[...])).astype(o_ref.dtype)


def _sb_prompt(proj, b, t):
    nqb = t // QBLK
    nhp = SB_HEADS * SB_DH // LANES
    return pl.pallas_call(
        _sb_prompt_kernel,
        grid=(b, nhp, nqb),
        in_specs=[
            pl.BlockSpec((QBLK, LANES), lambda bi, hp, qi: (bi * nqb + qi, CB_QSB + hp)),
            pl.BlockSpec((t, LANES), lambda bi, hp, qi: (bi, CB_KSB + hp)),
            pl.BlockSpec((t, LANES), lambda bi, hp, qi: (bi, CB_VSB + hp)),
            pl.BlockSpec((QBLK, LANES), lambda bi, hp, qi: (bi * nqb + qi, CB_ZSB + hp)),
        ],
        out_specs=pl.BlockSpec((QBLK, LANES), lambda bi, hp, qi: (bi * nqb + qi, hp)),
        out_shape=jax.ShapeDtypeStruct((b * t, SB_HEADS * SB_DH), BF16),
        compiler_params=_cparams(("arbitrary", "arbitrary", "arbitrary")),
        name="sb_prompt",
    )(proj, proj, proj, proj)


def _mem_prompt_kernel(q_ref, kv_ref, z_ref, o_ref):
    q = q_ref[...]
    z = z_ref[...]
    hw = MEM_HEADS * DH
    for h in range(MEM_HEADS):
        sl = slice(h * DH, (h + 1) * DH)
        s = _bdot_nt(q[:, sl], kv_ref[:, h * DH:(h + 1) * DH]) * MEM_SCALE
        p = _softmax_rows(s)
        o = _bdot(p, kv_ref[:, hw + h * DH:hw + (h + 1) * DH])
        o_ref[:, sl] = (o * _silu(z[:, sl])).astype(o_ref.dtype)


def _mem_prompt(proj, memkv, b, t, n_mem):
    nqb = t // QBLK
    hw = MEM_HEADS * DH
    return pl.pallas_call(
        _mem_prompt_kernel,
        grid=(b, nqb),
        in_specs=[
            pl.BlockSpec((QBLK, hw), lambda bi, qi: (bi * nqb + qi, CB_QMEM * LANES // hw)),
            pl.BlockSpec((n_mem, 2 * hw), lambda bi, qi: (bi, 0)),
            pl.BlockSpec((QBLK, hw), lambda bi, qi: (bi * nqb + qi, CB_ZMEM * LANES // hw)),
        ],
        out_specs=pl.BlockSpec((QBLK, hw), lambda bi, qi: (bi * nqb + qi, 0)),
        out_shape=jax.ShapeDtypeStruct((b * t, hw), BF16),
        compiler_params=_cparams(("arbitrary", "arbitrary")),
        name="mem_prompt",
    )(proj, memkv, proj)


def _merge_kernel(un_ref, us_ref, um_ref, wn_ref, ws_ref, wm_ref, g0_ref, g1_ref, g2_ref, wo_ref,
                  x_ref, gf_ref, y_ref, acc_ref):
    n = pl.program_id(1)

    @pl.when(n == 0)
    def _():
        acc_ref[...] = jnp.zeros_like(acc_ref)

    mixed = (_sigmoid(g0_ref[...]) * _bdot(un_ref[...], wn_ref[...])
             + _sigmoid(g1_ref[...]) * _bdot(us_ref[...], ws_ref[...])
             + _sigmoid(g2_ref[...]) * _bdot(um_ref[...], wm_ref[...]))
    acc_ref[...] += _bdot(mixed, wo_ref[...])

    @pl.when(n == pl.num_programs(1) - 1)
    def _():
        xo = x_ref[...] + acc_ref[...]
        ms = jnp.mean(xo * xo, axis=-1, keepdims=True)
        y_ref[...] = xo * lax.rsqrt(ms + RMS_EPS) * gf_ref[...]


def _merge(u_nsa, u_sb, u_mem, proj, x2d, w_br_nsa, w_br_sb, w_br_mem, w_out, g_final, tm, tn):
    r, d = x2d.shape
    gmrg0 = CB_GMRG * LANES // tn
    per = d // tn
    g_spec = lambda k: pl.BlockSpec((tm, tn), lambda i, n: (i, gmrg0 + k * per + n))
    row_spec = lambda w: pl.BlockSpec((tm, w), lambda i, n: (i, 0))
    return pl.pallas_call(
        _merge_kernel,
        grid=(r // tm, per),
        in_specs=[
            row_spec(u_nsa.shape[1]), row_spec(u_sb.shape[1]), row_spec(u_mem.shape[1]),
            pl.BlockSpec((w_br_nsa.shape[0], tn), lambda i, n: (0, n)),
            pl.BlockSpec((w_br_sb.shape[0], tn), lambda i, n: (0, n)),
            pl.BlockSpec((w_br_mem.shape[0], tn), lambda i, n: (0, n)),
            g_spec(0), g_spec(1), g_spec(2),
            pl.BlockSpec((tn, d), lambda i, n: (n, 0)),
            row_spec(d),
            pl.BlockSpec((1, d), lambda i, n: (0, 0)),
        ],
        out_specs=row_spec(d),
        out_shape=jax.ShapeDtypeStruct((r, d), F32),
        scratch_shapes=[pltpu.VMEM((tm, d), F32)],
        compiler_params=_cparams(("arbitrary", "arbitrary")),
        name="merge",
    )(u_nsa, u_sb, u_mem, w_br_nsa, w_br_sb, w_br_mem, proj, proj, proj, w_out, x2d, g_final.reshape(1, d))


def _cmp_sample_kernel(q_ref, kvc_ref, o_ref, sc_ref, *, past, n_cand):
    q = q_ref[...]
    ncp = kvc_ref.shape[1]
    npb = past // SEL_LEN
    ratio = SEL_LEN // CMP_STRIDE
    rr = lax.broadcasted_iota(jnp.int32, (ncp, n_cand), 0)
    jj = lax.broadcasted_iota(jnp.int32, (ncp, n_cand), 1)
    overlap = ((rr >= 1) & (rr >= ratio * jj) & (rr <= ratio * jj + ratio)).astype(BF16)
    ridx = lax.broadcasted_iota(jnp.int32, (1, ncp), 1)
    rmask = (ridx >= 1) & ((ridx - 1) * CMP_STRIDE + (CMP_LEN - 1) <= past)
    jl = lax.broadcasted_iota(jnp.int32, (1, n_cand), 1)
    cur = past // SEL_LEN
    valid = (jl * SEL_LEN <= past) & (jl <= npb)
    forced = (jl == 0) | (jl == cur) | (jl == cur - 1)
    for g in range(NSA_GROUPS):
        qg = q[g * NSA_HPG:(g + 1) * NSA_HPG]
        s = _bdot_nt(qg, kvc_ref[g]) * NSA_SCALE
        p = jnp.where(rmask, _softmax_rows(jnp.where(rmask, s, NEG)), 0.0)
        o_ref[g * NSA_HPG:(g + 1) * NSA_HPG, :] = _bdot(p, kvc_ref[NSA_GROUPS + g])
        p_grp = jnp.sum(p, axis=0, keepdims=True)
        p_slc = sum(jnp.dot(piece, overlap, preferred_element_type=F32) for piece in _split3(p_grp))
        sc_ref[g:g + 1, :] = jnp.where(valid, jnp.where(forced, FORCE_SCORE, p_slc), -jnp.inf)


def _cmp_sample(q_s, kvc_s, past, n_cand):
    b = q_s.shape[0]
    ncp = kvc_s.shape[2]
    return pl.pallas_call(
        functools.partial(_cmp_sample_kernel, past=past, n_cand=n_cand),
        grid=(b,),
        in_specs=[
            pl.BlockSpec((None, NSA_HEADS, DH), lambda bi: (bi, 0, 0)),
            pl.BlockSpec((None, 2 * NSA_GROUPS, ncp, DH), lambda bi: (bi, 0, 0, 0)),
        ],
        out_specs=[
            pl.BlockSpec((None, NSA_HEADS, DH), lambda bi: (bi, 0, 0)),
            pl.BlockSpec((None, NSA_GROUPS, n_cand), lambda bi: (bi, 0, 0)),
        ],
        out_shape=[jax.ShapeDtypeStruct((b, NSA_HEADS, DH), F32),
                   jax.ShapeDtypeStruct((b, NSA_GROUPS, n_cand), F32)],
        compiler_params=_cparams(("arbitrary",)),
        name="cmp_sample",
    )(q_s, kvc_s)


def _topk_kernel(sc_ref, idx_ref, val_ref, *, k_sel):
    sc = sc_ref[...]
    lane = lax.broadcasted_iota(jnp.int32, sc.shape, 1).astype(F32)
    out_lane = lax.broadcasted_iota(jnp.int32, idx_ref.shape, 1)
    idx = jnp.zeros(idx_ref.shape, F32)
    val = jnp.zeros(idx_ref.shape, F32)
    for k in range(k_sel):
        m = jnp.max(sc, axis=1, keepdims=True)
        am = jnp.min(jnp.where(sc == m, lane, float(sc.shape[1])), axis=1, keepdims=True)
        idx = jnp.where(out_lane == k, am, idx)
        val = jnp.where(out_lane == k, (m > -jnp.inf).astype(F32), val)
        sc = jnp.where(lane == am, -jnp.inf, sc)
    idx_ref[...] = idx.astype(jnp.int32)
    val_ref[...] = val.astype(jnp.int32)


def _topk(scores, k_sel):
    n = scores.shape[0]
    return pl.pallas_call(
        functools.partial(_topk_kernel, k_sel=k_sel),
        out_shape=[jax.ShapeDtypeStruct((n, LANES), jnp.int32), jax.ShapeDtypeStruct((n, LANES), jnp.int32)],
        name="topk_sample",
    )(scores)


def _attn_sample_kernel(idx_ref, val_ref, pt_ref, *refs, past, k_sel, wb):
    n_blk_refs = NSA_GROUPS * k_sel
    blk_refs = refs[:n_blk_refs]
    (q_ref, kvs_new_ref, win_ref, kvw_new_ref, ocmp_ref, g_ref, z_ref,
     qm_ref, mem_ref, zm_ref, on_ref, om_ref) = refs[n_blk_refs:]
    bi = pl.program_id(0)
    npb = past // SEL_LEN
    q = q_ref[...]
    rpp = 2 * NSA_GROUPS
    gs = _sigmoid(g_ref[...])
    lane = lax.broadcasted_iota(jnp.int32, gs.shape, 1)
    hrow = lax.broadcasted_iota(jnp.int32, (NSA_HPG, LANES), 0)
    row64 = lax.broadcasted_iota(jnp.int32, (SEL_LEN, DH), 0)
    kl = lax.broadcasted_iota(jnp.int32, (1, SEL_LEN), 1)
    for g in range(NSA_GROUPS):
        qg = q[g * NSA_HPG:(g + 1) * NSA_HPG]
        k_new = kvs_new_ref[g:g + 1, :]
        v_new = kvs_new_ref[NSA_GROUPS + g:NSA_GROUPS + g + 1, :]
        ks, vs, masks = [], [], []
        for k in range(k_sel):
            ref = blk_refs[g * k_sel + k]
            flat = (bi * NSA_GROUPS + g) * k_sel + k
            bidx = idx_ref[flat]
            is_tail = bidx >= npb
            kb = ref[pl.ds(g, SEL_LEN, stride=rpp), :]
            vb = ref[pl.ds(NSA_GROUPS + g, SEL_LEN, stride=rpp), :]
            ks.append(jnp.where(is_tail, jnp.where(row64 == 0, k_new, 0.0), kb))
            vs.append(jnp.where(is_tail, jnp.where(row64 == 0, v_new, 0.0), vb))
            masks.append(((val_ref[flat] > 0) & (bidx * SEL_LEN + kl <= past)).astype(F32))
        k_all = jnp.concatenate(ks, axis=0)
        v_all = jnp.concatenate(vs, axis=0)
        mask = jnp.concatenate(masks, axis=1) > 0.5
        s = _bdot_nt(qg, k_all) * NSA_SCALE
        o_sel = _bdot(_softmax_rows(jnp.where(mask, s, NEG)), v_all)
        kw = win_ref[pl.ds(g, wb, stride=rpp), :]
        vw = win_ref[pl.ds(NSA_GROUPS + g, wb, stride=rpp), :]
        kw_new = kvw_new_ref[g:g + 1, :]
        vw_new = kvw_new_ref[NSA_GROUPS + g:NSA_GROUPS + g + 1, :]
        kposw = past - wb + lax.broadcasted_iota(jnp.int32, (1, wb), 1)
        wmask = (kposw <= past) & (past - kposw < WINDOW) & (kposw >= 0)
        sw = jnp.where(wmask, _bdot_nt(qg, kw) * NSA_SCALE, NEG)
        sw_new = jnp.sum(qg.astype(BF16).astype(F32) * kw_new.astype(BF16).astype(F32),
                         axis=1, keepdims=True) * NSA_SCALE
        mw = jnp.maximum(jnp.max(sw, axis=1, keepdims=True), sw_new)
        ew = jnp.exp(sw - mw)
        ew_new = jnp.exp(sw_new - mw)
        den = jnp.sum(ew, axis=1, keepdims=True) + ew_new
        pw = ew / den
        pw_new = (ew_new / den).astype(BF16).astype(F32)
        o_win = _bdot(pw, vw) + pw_new * vw_new.astype(BF16).astype(F32)
        o_cmp = ocmp_ref[g * NSA_HPG:(g + 1) * NSA_HPG, :]
        o_g = jnp.zeros((NSA_HPG, DH), F32)
        for br, o_br in enumerate((o_cmp, o_sel, o_win)):
            gate = jnp.zeros((NSA_HPG, 1), F32)
            for h in range(NSA_HPG):
                col = br * NSA_HEADS + g * NSA_HPG + h
                gh = jnp.sum(jnp.where(lane == col, gs, 0.0), axis=1, keepdims=True)
                gate = jnp.where(hrow[:, 0:1] == h, gh, gate)
            o_g = o_g + gate * o_br
        on_ref[g * NSA_HPG:(g + 1) * NSA_HPG, :] = o_g * _silu(z_ref[g * NSA_HPG:(g + 1) * NSA_HPG, :])

    qm = qm_ref[...]
    n_mem = mem_ref.shape[0] // (2 * MEM_HEADS)
    mrow = lax.broadcasted_iota(jnp.int32, (MEM_HEADS, DH), 0)
    om = jnp.zeros((MEM_HEADS, DH), F32)
    for h in range(MEM_HEADS):
        kh = mem_ref[pl.ds(h, n_mem, stride=2 * MEM_HEADS), :]
        vh = mem_ref[pl.ds(MEM_HEADS + h, n_mem, stride=2 * MEM_HEADS), :]
        p = _softmax_rows(_bdot_nt(qm, kh) * MEM_SCALE)
        om = jnp.where(mrow == h, _bdot(p, vh), om)
    om_ref[...] = om * _silu(zm_ref[...])


def _attn_sample(idx, val, page_table, sel2d, q_s, kvs_new, win2d, kvw_new, o_cmp, g_s, z_s,
                 qm_s, mem2d, zm_s, past, k_sel):
    b = q_s.shape[0]
    npb = past // SEL_LEN
    bpp = PAGE // SEL_LEN
    rows_blk = SEL_LEN * 2 * NSA_GROUPS
    wb = win2d.shape[1] // (2 * NSA_GROUPS)

    def blk_spec(g, k):
        def imap(bi, idx_r, val_r, pt_r):
            j = jnp.minimum(idx_r[(bi * NSA_GROUPS + g) * k_sel + k], npb - 1)
            return (pt_r[bi, j // bpp] * bpp + j % bpp, 0, 0)
        return pl.BlockSpec((None, rows_blk, LANES), imap)

    def per_b(shape):
        return pl.BlockSpec((None,) + shape, lambda bi, idx_r, val_r, pt_r: (bi,) + (0,) * len(shape))

    grid_spec = pltpu.PrefetchScalarGridSpec(
        num_scalar_prefetch=3,
        grid=(b,),
        in_specs=[blk_spec(g, k) for g in range(NSA_GROUPS) for k in range(k_sel)] + [
            per_b((NSA_HEADS, DH)), per_b((2 * NSA_GROUPS, DH)), per_b(win2d.shape[1:]),
            per_b((2 * NSA_GROUPS, DH)), per_b((NSA_HEADS, DH)), per_b((1, LANES)), per_b((NSA_HEADS, DH)),
            per_b((MEM_HEADS, DH)), per_b(mem2d.shape[1:]), per_b((MEM_HEADS, DH)),
        ],
        out_specs=[per_b((NSA_HEADS, DH)), per_b((MEM_HEADS, DH))],
    )
    return pl.pallas_call(
        functools.partial(_attn_sample_kernel, past=past, k_sel=k_sel, wb=wb),
        grid_spec=grid_spec,
        out_shape=[jax.ShapeDtypeStruct((b, NSA_HEADS, DH), F32), jax.ShapeDtypeStruct((b, MEM_HEADS, DH), F32)],
        compiler_params=_cparams(("arbitrary",)),
        name="attn_sample",
    )(idx, val, page_table, *([sel2d] * (NSA_GROUPS * k_sel)), q_s, kvs_new, win2d, kvw_new, o_cmp, g_s, z_s,
      qm_s, mem2d, zm_s)


def _sb_sample_kernel(pt_ref, *refs, n_pg, past):
    page_refs = refs[:n_pg]
    qt_ref, q_ref, kn_ref, vn_ref, z_ref, o_ref, c_ref, acc_ref = refs[n_pg:]
    step = pl.program_id(1)
    n_steps = pl.num_programs(1)
    hrow = lax.broadcasted_iota(jnp.int32, (SB_HEADS, SB_DH), 0)
    jr = lax.broadcasted_iota(jnp.int32, (PAGE, PAGE), 0)
    kc = lax.broadcasted_iota(jnp.int32, (PAGE, PAGE), 1)
    upper = (jr > kc).astype(BF16)

    @pl.when(step == 0)
    def _():
        kn = kn_ref[...]
        zn = jnp.sum(q_ref[...].astype(BF16).astype(F32) * kn.astype(BF16).astype(F32),
                     axis=1, keepdims=True) * SB_SCALE
        visible = jnp.full(zn.shape, past, jnp.int32) < past
        ls = _log_sigmoid(zn)
        mn = jnp.where(visible, ls - zn, 0.0)
        an = jnp.where(visible, jnp.exp(ls), 0.0)
        c_ref[...] = mn
        acc_ref[...] = an.astype(BF16).astype(F32) * vn_ref[...].astype(BF16).astype(F32)

    alive = jnp.max(c_ref[...]) > SB_DEAD

    @pl.when(alive)
    def _():
        c = c_ref[...]
        acc = acc_ref[...]
        for k in range(n_pg):
            ref = page_refs[k]
            rows = []
            for h in range(SB_HEADS):
                qh = qt_ref[h].astype(BF16).astype(F32)
                kt = ref[h].astype(BF16).astype(F32)
                rows.append(jnp.sum(qh * kt, axis=0, keepdims=True))
            z = jnp.concatenate(rows, axis=0) * SB_SCALE
            ls = _log_sigmoid(z)
            m = ls - z
            m_hi = m.astype(BF16)
            m_lo = (m - m_hi.astype(F32)).astype(BF16)
            excl = (jnp.dot(m_hi, upper, preferred_element_type=F32)
                    + jnp.dot(m_lo, upper, preferred_element_type=F32))
            a = jnp.exp(ls + excl + c)
            for h in range(SB_HEADS):
                oh = _bdot_nt(a, ref[SB_HEADS + h])
                acc = acc + jnp.where(hrow == h, oh, 0.0)
            c = c + jnp.sum(m, axis=1, keepdims=True)
        c_ref[...] = c
        acc_ref[...] = acc

    @pl.when(step == n_steps - 1)
    def _():
        o_ref[...] = acc_ref[...] * _silu(z_ref[...])


def _sb_sample(sbt, page_table, qt_s, q_s, kn_s, vn_s, z_s, past, n_pg):
    b, n_pages = page_table.shape

    def page_spec(k):
        return pl.BlockSpec((None, 2 * SB_HEADS, SB_DH, PAGE),
                            lambda bi, s, pt: (pt[bi, n_pages - 1 - (s * n_pg + k)], 0, 0, 0))

    def per_b(shape):
        return pl.BlockSpec((None,) + shape, lambda bi, s, pt: (bi,) + (0,) * len(shape))

    grid_spec = pltpu.PrefetchScalarGridSpec(
        num_scalar_prefetch=1,
        grid=(b, n_pages // n_pg),
        in_specs=[page_spec(k) for k in range(n_pg)] + [
            per_b((SB_HEADS, SB_DH, 1)), per_b((SB_HEADS, SB_DH)), per_b((SB_HEADS, SB_DH)),
            per_b((SB_HEADS, SB_DH)), per_b((SB_HEADS, SB_DH)),
        ],
        out_specs=per_b((SB_HEADS, SB_DH)),
        scratch_shapes=[pltpu.VMEM((SB_HEADS, 1), F32), pltpu.VMEM((SB_HEADS, SB_DH), F32)],
    )
    return pl.pallas_call(
        functools.partial(_sb_sample_kernel, n_pg=n_pg, past=past),
        grid_spec=grid_spec,
        out_shape=jax.ShapeDtypeStruct((b, SB_HEADS, SB_DH), F32),
        compiler_params=_cparams(("arbitrary", "arbitrary")),
        name="sb_sample",
    )(page_table, *([sbt] * n_pg), qt_s, q_s, kn_s, vn_s, z_s)


def _win_update_kernel(w_ref, new_ref, o_ref):
    n_new = new_ref.shape[0]
    n = w_ref.shape[0]
    o_ref[0:n - n_new, :] = w_ref[n_new:n, :]
    o_ref[n - n_new:n, :] = new_ref[...]


def _win_update(win2d, kvw_new):
    b, n, _ = win2d.shape
    return pl.pallas_call(
        _win_update_kernel,
        grid=(b,),
        in_specs=[pl.BlockSpec((None, n, LANES), lambda bi: (bi, 0, 0)),
                  pl.BlockSpec((None, kvw_new.shape[1], LANES), lambda bi: (bi, 0, 0))],
        out_specs=pl.BlockSpec((None, n, LANES), lambda bi: (bi, 0, 0)),
        out_shape=jax.ShapeDtypeStruct(win2d.shape, F32),
        compiler_params=_cparams(("arbitrary",)),
        name="win_update",
    )(win2d, kvw_new)


def _pick_tile(n, pref):
    while n % pref:
        pref //= 2
    return pref


def _layer(xp, xs, mem, cache_cmp, cache_sel, cache_sb, cache_win, cache_mem, page_table,
           g_norm, w_in, pe_cmp, w1_cmp, b1_cmp, w2_cmp, g_mem, w_mem_kv,
           w_br_nsa, w_br_sb, w_br_mem, w_out, g_final):
    b, t, d = xp.shape
    bs = xs.shape[0]
    n_pool = cache_cmp.shape[0]
    n_pages = page_table.shape[1]
    past = n_pages * PAGE
    n_mem = mem.shape[1]

    w_main = jnp.concatenate([w_in[:, :G_NSA_COL0], w_in[:, G_NSA_COL0 + N_GNSA:],
                              w_in[:, G_NSA_COL0:G_NSA_COL0 + N_GNSA],
                              jnp.zeros((d, PROJ_TN - N_GNSA), w_in.dtype)], axis=1).astype(BF16)
    pe2 = pe_cmp.transpose(1, 0, 2).reshape(2, CMP_STRIDE, 2 * DH)
    w1v = w1_cmp.reshape(2, CMP_LEN * DH, CMP_HID).astype(BF16)
    b1 = b1_cmp.reshape(2, 1, CMP_HID)
    w2 = w2_cmp.astype(BF16)
    rope_chunks = (4, 4, 2, 2, 2)

    x2d = xp.reshape(b * t, d)
    proj = _project(x2d, g_norm, w_main, _rope_tables(jnp.arange(t, dtype=jnp.int32)), rope_chunks,
                    _pick_tile(t, 1024), PROJ_TN)
    kvc = _compress_prompt(proj, b, t, pe2, w1v, b1, w2)
    u_nsa = _nsa_prompt(proj, kvc, b, t)
    u_sb = _sb_prompt(proj, b, t)
    ones_tab = jnp.ones((b * n_mem, LANES), F32)
    memkv = _project(mem.reshape(b * n_mem, d), g_mem, w_mem_kv.astype(BF16), (ones_tab, ones_tab, ones_tab),
                     (), _pick_tile(b * n_mem, 512), PROJ_TN)
    u_mem = _mem_prompt(proj, memkv, b, t, n_mem)
    wbn, wbs, wbm, wo = (w.astype(BF16) for w in (w_br_nsa, w_br_sb, w_br_mem, w_out))
    y_p = _merge(u_nsa, u_sb, u_mem, proj, x2d, wbn, wbs, wbm, wo, g_final, _pick_tile(b * t, 512), 512)

    proj3 = proj.reshape(b, t, PROJ_W)
    kv_w = 2 * NSA_GROUPS * DH

    def kv_out(p3, cb):
        return p3[:, :, cb * LANES:cb * LANES + kv_w].reshape(p3.shape[0], p3.shape[1], 2, NSA_GROUPS, DH)

    def sb_out(p3):
        n, tt = p3.shape[:2]
        k = p3[:, :, CB_KSB * LANES:CB_VSB * LANES].reshape(n, tt, SB_HEADS, SB_DH)
        v = p3[:, :, CB_VSB * LANES:CB_ZSB * LANES].reshape(n, tt, SB_HEADS, SB_DH)
        return jnp.stack([k, v], axis=2)

    p_cmp, p_sel, p_sb = kv_out(proj3, CB_KVC), kv_out(proj3, CB_KVS), sb_out(proj3)
    p_win = kv_out(proj3, CB_KVW)[:, t - min(WINDOW, t):]
    p_mem = memkv.reshape(b, n_mem, 2, MEM_HEADS, DH)

    xs2d = xs.reshape(bs, d)
    pos_s = jnp.full((bs,), past, jnp.int32)
    proj_s = _project(xs2d, g_norm, w_main, _rope_tables(pos_s), rope_chunks, bs, PROJ_TN)
    sec = lambda cb, n: proj_s[:, cb * LANES:(cb + n) * LANES]
    q_s = sec(CB_Q, 8).reshape(bs, NSA_HEADS, DH)
    kvs_new = sec(CB_KVS, 4).reshape(bs, 2 * NSA_GROUPS, DH)
    kvw_new = sec(CB_KVW, 4).reshape(bs, 2 * NSA_GROUPS, DH)
    z_s = sec(CB_ZNSA, 8).reshape(bs, NSA_HEADS, DH)
    g_s = sec(CB_GNSA, 1).reshape(bs, 1, LANES)
    qsb_s = sec(CB_QSB, 4).reshape(bs, SB_HEADS, SB_DH)
    ksb_s = sec(CB_KSB, 4).reshape(bs, SB_HEADS, SB_DH)
    vsb_s = sec(CB_VSB, 4).reshape(bs, SB_HEADS, SB_DH)
    zsb_s = sec(CB_ZSB, 4).reshape(bs, SB_HEADS, SB_DH)
    qm_s = sec(CB_QMEM, 4).reshape(bs, MEM_HEADS, DH)
    zm_s = sec(CB_ZMEM, 4).reshape(bs, MEM_HEADS, DH)

    cmp2d = cache_cmp.reshape(n_pool * PAGE * 2 * NSA_GROUPS, DH)
    sel2d = cache_sel.reshape(n_pool * (PAGE // SEL_LEN), SEL_LEN * 2 * NSA_GROUPS, DH)
    sbt = cache_sb.transpose(0, 2, 3, 4, 1).reshape(n_pool, 2 * SB_HEADS, SB_DH, PAGE)
    win2d = cache_win.reshape(bs, -1, DH)
    mem2d = cache_mem.reshape(bs, -1, DH)

    n_pg_cmp = _pick_tile(n_pages, 32)
    kvc_s = _compress_pages(cmp2d, page_table, pe2, w1v, b1, w2, n_pg_cmp)
    n_blk = past // SEL_LEN + 1
    k_sel = min(N_SEL, n_blk)
    n_cand = -(-n_blk // LANES) * LANES
    o_cmp_s, scores = _cmp_sample(q_s, kvc_s, past, n_cand)
    idx, val = _topk(scores.reshape(bs * NSA_GROUPS, n_cand), k_sel)
    idx_flat = idx[:, :k_sel].reshape(-1)
    val_flat = val[:, :k_sel].reshape(-1)
    un_s, um_s = _attn_sample(idx_flat, val_flat, page_table, sel2d, q_s, kvs_new, win2d, kvw_new,
                              o_cmp_s, g_s, z_s, qm_s, mem2d, zm_s, past, k_sel)
    us_s = _sb_sample(sbt, page_table, qsb_s.reshape(bs, SB_HEADS, SB_DH, 1), qsb_s, ksb_s, vsb_s, zsb_s,
                      past, _pick_tile(n_pages, 8))
    y_s = _merge(un_s.reshape(bs, -1), us_s.reshape(bs, -1), um_s.reshape(bs, -1), proj_s, xs2d,
                 wbn, wbs, wbm, wo, g_final, bs, 512)
    s_win = _win_update(win2d, kvw_new)

    ps3 = proj_s.reshape(bs, 1, PROJ_W)
    outs_p = (y_p.reshape(b, t, d), p_cmp, p_sel, p_sb, p_win, p_mem)
    outs_s = (y_s.reshape(bs, 1, d), kv_out(ps3, CB_KVC), kv_out(ps3, CB_KVS), sb_out(ps3),
              s_win.reshape(cache_win.shape))
    return outs_p, outs_s


def kernel(x_prompt, x_sample, mem_prompt, cache_cmp_kv, cache_sel_kv, cache_sb_kv, cache_win_kv, cache_mem_kv, page_table, g_norm, w_in, pe_cmp, w1_cmp, b1_cmp, w2_cmp, g_mem, w_mem_kv, w_br_nsa, w_br_sb, w_br_mem, w_out, g_final):
    depth = g_norm.shape[0]
    assert depth == 1 and x_sample.shape[1] == 1, "single layer, one new token per sequence"
    l = 0
    outs_p, outs_s = _layer(
        x_prompt, x_sample, mem_prompt, cache_cmp_kv[l], cache_sel_kv[l], cache_sb_kv[l], cache_win_kv[l],
        cache_mem_kv[l], page_table, g_norm[l], w_in[l], pe_cmp[l], w1_cmp[l], b1_cmp[l], w2_cmp[l],
        g_mem[l], w_mem_kv[l], w_br_nsa[l], w_br_sb[l], w_br_mem[l], w_out[l], g_final)
    y_p, p_cmp, p_sel, p_sb, p_win, p_mem = outs_p
    y_s, s_cmp, s_sel, s_sb, s_win = outs_s
    return (y_p, y_s, p_cmp[None], p_sel[None], p_sb[None], p_win[None], p_mem[None],
            s_cmp[None], s_sel[None], s_sb[None], s_win[None])
```
